```python
import math
import jax
import jax.numpy as jnp
from jax import lax
import numpy as np

D_MODEL = 4096
BATCH = 4
SEQ = 4096
DEPTH = 1

CHUNK = 64
QBLOCK = 128
HEAD_DIM = 128
WIDTH_A = D_MODEL // 2
WIDTH_B = D_MODEL - WIDTH_A
MIX_WIDTH = WIDTH_A + WIDTH_B
N_DIFF_HEADS = WIDTH_A // (2 * HEAD_DIM)
N_BAND_HEADS = WIDTH_B // HEAD_DIM
LEFT_CHUNKS = 8
BAND = LEFT_CHUNKS + 1
REL_CLIP = 256
N_REL = 2 * REL_CLIP + 1
ROPE_THETA = 10000.0
MEM_LEN = 256
N_MEM_HEADS = 4
MEM_HEAD_DIM = D_MODEL // N_MEM_HEADS
D_FF = ((8 * D_MODEL // 3 + 255) // 256) * 256
CONV_W = 3
IN_COLS = 3 * WIDTH_A + 3 * WIDTH_B
LAMBDA_STD = 0.1
REL_BIAS_STD = 0.2
EPS = 1e-6

kernel_name = "hybrid_diffattn_chunkband_streaming_layer"


def rms_norm(x, w):
    xf = x.astype(jnp.float32)
    y = xf * lax.rsqrt(jnp.mean(xf * xf, axis=-1, keepdims=True) + EPS)
    return (y * w.astype(jnp.float32)).astype(x.dtype)


def rope_tables(seq, dim, dtype):
    inv = 1.0 / (ROPE_THETA ** (jnp.arange(0, dim, 2, dtype=jnp.float32) / dim))
    ang = jnp.arange(seq, dtype=jnp.float32)[:, None] * inv[None, :]
    ang = jnp.concatenate([ang, ang], axis=-1)
    return jnp.cos(ang).astype(dtype), jnp.sin(ang).astype(dtype)


def apply_rope(t, cos, sin):
    t1, t2 = jnp.split(t, 2, axis=-1)
    return t * cos + jnp.concatenate([-t2, t1], axis=-1) * sin


def diff_attention(q, k, v, lam, subln_w, lambda_init):
    B, H2, S, Dh = q.shape
    H = H2 // 2
    nqb = S // QBLOCK
    scale = Dh ** -0.5
    qb = q.reshape(B, H2, nqb, QBLOCK, Dh).transpose(2, 0, 1, 3, 4)
    key_chunk = jnp.arange(S) // CHUNK

    def block(args):
        qi, i = args
        s = jnp.einsum('bhqd,bhkd->bhqk', qi, k).astype(jnp.float32) * scale
        q_chunk = (i * QBLOCK + jnp.arange(QBLOCK)) // CHUNK
        mask = key_chunk[None, :] <= q_chunk[:, None]
        s = jnp.where(mask[None, None], s, -jnp.inf)
        p = jax.nn.softmax(s, axis=-1).reshape(B, H, 2, QBLOCK, S)
        a = p[:, :, 0] - lam * p[:, :, 1]
        return jnp.einsum('bhqk,bhkd->bhqd', a.astype(v.dtype), v)

    o = lax.map(block, (qb, jnp.arange(nqb)))
    o = o.transpose(1, 2, 0, 3, 4).reshape(B, H, S, 2 * Dh)
    return rms_norm(o, subln_w) * (1.0 - lambda_init)


def chunk_band_attention(q, k, v, rel_bias):
    B, H, S, Dh = q.shape
    nc = S // CHUNK
    scale = Dh ** -0.5
    qc = q.reshape(B, H, nc, CHUNK, Dh)
    pad = ((0, 0), (0, 0), (LEFT_CHUNKS, 0), (0, 0), (0, 0))
    kc = jnp.pad(k.reshape(B, H, nc, CHUNK, Dh), pad)
    vc = jnp.pad(v.reshape(B, H, nc, CHUNK, Dh), pad)
    off = jnp.arange(BAND)[:, None, None]
    rel = (off - LEFT_CHUNKS) * CHUNK + jnp.arange(CHUNK)[None, None, :] - jnp.arange(CHUNK)[None, :, None]
    rel = jnp.clip(rel, -REL_CLIP, REL_CLIP) + REL_CLIP
    bias = rel_bias.astype(jnp.float32)[:, rel]
    bias = bias.transpose(0, 2, 1, 3).reshape(H, CHUNK, BAND * CHUNK)
    valid = (jnp.arange(nc)[:, None] + jnp.arange(BAND)[None, :] - LEFT_CHUNKS) >= 0
    valid = jnp.repeat(valid, CHUNK, axis=1)
    s = jnp.concatenate(
        [jnp.einsum('bhcqd,bhckd->bhcqk', qc, kc[:, :, j:j + nc]) for j in range(BAND)], axis=-1
    ).astype(jnp.float32) * scale
    s = s + bias[None, :, None]
    s = jnp.where(valid[None, None, :, None, :], s, -jnp.inf)
    p = jax.nn.softmax(s, axis=-1).astype(v.dtype).reshape(B, H, nc, CHUNK, BAND, CHUNK)
    o = sum(jnp.einsum('bhcqk,bhckd->bhcqd', p[:, :, :, :, j], vc[:, :, j:j + nc]) for j in range(BAND))
    return o.reshape(B, H, S, Dh)


def parallel_mixer(h, w_in, lq1, lk1, lq2, lk2, subln_w, rel_bias, w_out, cos, sin, lambda_init):
    B, S, _ = h.shape
    proj = h @ w_in
    splits = np.cumsum([WIDTH_A, WIDTH_A, WIDTH_A, WIDTH_B, WIDTH_B]).tolist()
    qa, ka, va, qb, kb, vb = jnp.split(proj, splits, axis=-1)

    def heads(t, n, d):
        return t.reshape(B, S, n, d).transpose(0, 2, 1, 3)

    qa = apply_rope(heads(qa, 2 * N_DIFF_HEADS, HEAD_DIM), cos, sin)
    ka = apply_rope(heads(ka, 2 * N_DIFF_HEADS, HEAD_DIM), cos, sin)
    va = heads(va, N_DIFF_HEADS, 2 * HEAD_DIM)
    f32 = jnp.float32
    lam = (jnp.exp(jnp.sum(lq1.astype(f32) * lk1.astype(f32)))
           - jnp.exp(jnp.sum(lq2.astype(f32) * lk2.astype(f32))) + lambda_init)
    oa = diff_attention(qa, ka, va, lam, subln_w, lambda_init)
    ob = chunk_band_attention(heads(qb, N_BAND_HEADS, HEAD_DIM), heads(kb, N_BAND_HEADS, HEAD_DIM),
                              heads(vb, N_BAND_HEADS, HEAD_DIM), rel_bias)
    oa = oa.transpose(0, 2, 1, 3).reshape(B, S, WIDTH_A)
    ob = ob.transpose(0, 2, 1, 3).reshape(B, S, WIDTH_B)
    return jnp.concatenate([oa, ob], axis=-1) @ w_out


def memory_cross_attention(h, mem_n, w_mq, w_mkv, w_mo):
    B, S, D = h.shape
    M = mem_n.shape[1]
    q = (h @ w_mq).reshape(B, S, N_MEM_HEADS, MEM_HEAD_DIM)
    k, v = jnp.split(mem_n @ w_mkv, 2, axis=-1)
    k = k.reshape(B, M, N_MEM_HEADS, MEM_HEAD_DIM)
    v = v.reshape(B, M, N_MEM_HEADS, MEM_HEAD_DIM)
    s = jnp.einsum('bqhd,bkhd->bhqk', q, k).astype(jnp.float32) * (MEM_HEAD_DIM ** -0.5)
    p = jax.nn.softmax(s, axis=-1).astype(v.dtype)
    o = jnp.einsum('bhqk,bkhd->bqhd', p, v).reshape(B, S, D)
    return o @ w_mo


def conv_glu_ffn(h, w_up, conv_w, conv_b, w_down):
    g, u = jnp.split(h @ w_up, 2, axis=-1)
    g = lax.conv_general_dilated(g, conv_w[:, None, :].astype(g.dtype), window_strides=(1,),
                                 padding=[(CONV_W - 1, 0)], dimension_numbers=('NWC', 'WIO', 'NWC'),
                                 feature_group_count=g.shape[-1]) + conv_b
    return (jax.nn.silu(g) * u) @ w_down


def setup_inputs(seed: int = 0) -> dict:
    key = jax.random.key(seed)
    ks = jax.random.split(key, 24)

    def nrm(k, shape, scale):
        return jax.random.normal(k, shape, jnp.float32) * scale

    def gain(k, shape):
        return 1.0 + 0.02 * jax.random.normal(k, shape, jnp.float32)

    return {
        "x": nrm(ks[0], (BATCH, SEQ, D_MODEL), 1.0),
        "mem": nrm(ks[1], (BATCH, MEM_LEN, D_MODEL), 1.0),
        "norm_mix_w": gain(ks[2], (DEPTH, D_MODEL)),
        "w_in": nrm(ks[3], (DEPTH, D_MODEL, IN_COLS), D_MODEL ** -0.5),
        "lambda_q1": nrm(ks[4], (DEPTH, HEAD_DIM), LAMBDA_STD),
        "lambda_k1": nrm(ks[5], (DEPTH, HEAD_DIM), LAMBDA_STD),
        "lambda_q2": nrm(ks[6], (DEPTH, HEAD_DIM), LAMBDA_STD),
        "lambda_k2": nrm(ks[7], (DEPTH, HEAD_DIM), LAMBDA_STD),
        "subln_w": gain(ks[8], (DEPTH, 2 * HEAD_DIM)),
        "rel_bias": nrm(ks[9], (DEPTH, N_BAND_HEADS, N_REL), REL_BIAS_STD),
        "w_out": nrm(ks[10], (DEPTH, MIX_WIDTH, D_MODEL), MIX_WIDTH ** -0.5),
        "norm_xq_w": gain(ks[11], (DEPTH, D_MODEL)),
        "norm_mem_w": gain(ks[12], (DEPTH, D_MODEL)),
        "w_mq": nrm(ks[13], (DEPTH, D_MODEL, D_MODEL), D_MODEL ** -0.5),
        "w_mkv": nrm(ks[14], (DEPTH, D_MODEL, 2 * D_MODEL), D_MODEL ** -0.5),
        "w_mo": nrm(ks[15], (DEPTH, D_MODEL, D_MODEL), D_MODEL ** -0.5),
        "norm_ffn_w": gain(ks[16], (DEPTH, D_MODEL)),
        "w_up": nrm(ks[17], (DEPTH, D_MODEL, 2 * D_FF), D_MODEL ** -0.5),
        "conv_w": nrm(ks[18], (DEPTH, CONV_W, D_FF), CONV_W ** -0.5),
        "conv_b": nrm(ks[19], (DEPTH, D_FF), 0.02),
        "w_down": nrm(ks[20], (DEPTH, D_FF, D_MODEL), D_FF ** -0.5),
        "final_norm_w": gain(ks[21], (D_MODEL,)),
    }


def reference(x, mem, norm_mix_w, w_in, lambda_q1, lambda_k1, lambda_q2, lambda_k2, subln_w, rel_bias,
              w_out, norm_xq_w, norm_mem_w, w_mq, w_mkv, w_mo, norm_ffn_w, w_up, conv_w, conv_b, w_down,
              final_norm_w):
    S = x.shape[1]
    cos, sin = rope_tables(S, HEAD_DIM, x.dtype)
    for l in range(DEPTH):
        lambda_init = 0.8 - 0.6 * math.exp(-0.3 * l)
        x = x + parallel_mixer(rms_norm(x, norm_mix_w[l]), w_in[l], lambda_q1[l], lambda_k1[l],
                               lambda_q2[l], lambda_k2[l], subln_w[l], rel_bias[l], w_out[l],
                               cos, sin, lambda_init)
        x = x + memory_cross_attention(rms_norm(x, norm_xq_w[l]), rms_norm(mem, norm_mem_w[l]),
                                       w_mq[l], w_mkv[l], w_mo[l])
        x = x + conv_glu_ffn(rms_norm(x, norm_ffn_w[l]), w_up[l], conv_w[l], conv_b[l], w_down[l])
    return rms_norm(x, final_norm_w)
```

```python
import functools
import math

import jax
import jax.numpy as jnp
from jax import lax
from jax.experimental import pallas as pl
from jax.experimental.pallas import tpu as pltpu

F32 = jnp.float32
BF16 = jnp.bfloat16

CHUNK = 64
HEAD_DIM = 128
LEFT_CHUNKS = 8
REL_CLIP = 256
ROPE_THETA = 10000.0
N_MEM_HEADS = 4
CONV_W = 3
EPS = 1e-6

VMEM_LIMIT_BYTES = 56 * 1024 * 1024


def _params(*sem):
    return pltpu.CompilerParams(dimension_semantics=sem, vmem_limit_bytes=VMEM_LIMIT_BYTES)


def _rmsnorm_kernel(x_ref, w_ref, o_ref):
    x = x_ref[...].astype(F32)
    y = x * lax.rsqrt(jnp.mean(x * x, axis=-1, keepdims=True) + EPS)
    o_ref[...] = (y * w_ref[...]).astype(o_ref.dtype)


def rmsnorm(x, w, out_dtype, tr=512):
    m, d = x.shape
    return pl.pallas_call(
        _rmsnorm_kernel,
        grid=(m // tr,),
        in_specs=[pl.BlockSpec((tr, d), lambda i: (i, 0)),
                  pl.BlockSpec((1, d), lambda i: (0, 0))],
        out_specs=pl.BlockSpec((tr, d), lambda i: (i, 0)),
        out_shape=jax.ShapeDtypeStruct((m, d), out_dtype),
        compiler_params=_params("parallel"),
        name="rmsnorm",
    )(x, w.reshape(1, d).astype(F32))


def _mm_kernel(a_ref, w_ref, o_ref):
    o_ref[...] = jnp.dot(a_ref[...], w_ref[...], preferred_element_type=F32).astype(o_ref.dtype)


def matmul(a, w, out_dtype, tm=1024, tn=1024):
    m, k = a.shape
    n = w.shape[1]
    return pl.pallas_call(
        _mm_kernel,
        grid=(m // tm, n // tn),
        in_specs=[pl.BlockSpec((tm, k), lambda i, j: (i, 0)),
                  pl.BlockSpec((k, tn), lambda i, j: (0, j))],
        out_specs=pl.BlockSpec((tm, tn), lambda i, j: (i, j)),
        out_shape=jax.ShapeDtypeStruct((m, n), out_dtype),
        compiler_params=_params("parallel", "arbitrary"),
        name="matmul",
    )(a, w)


def _mm_res_kernel(a_ref, w_ref, r_ref, o_ref):
    k = pl.program_id(2)
    d = jnp.dot(a_ref[...], w_ref[...], preferred_element_type=F32)

    @pl.when(k == 0)
    def _():
        o_ref[...] = r_ref[...] + d

    @pl.when(k > 0)
    def _():
        o_ref[...] += d


def matmul_residual(a, w, res, tm=1024, tn=512, tk=None):
    m, k = a.shape
    n = w.shape[1]
    tk = k if tk is None else tk
    return pl.pallas_call(
        _mm_res_kernel,
        grid=(m // tm, n // tn, k // tk),
        in_specs=[pl.BlockSpec((tm, tk), lambda i, j, kk: (i, kk)),
                  pl.BlockSpec((tk, tn), lambda i, j, kk: (kk, j)),
                  pl.BlockSpec((tm, tn), lambda i, j, kk: (i, j))],
        out_specs=pl.BlockSpec((tm, tn), lambda i, j, kk: (i, j)),
        out_shape=jax.ShapeDtypeStruct((m, n), F32),
        compiler_params=_params("parallel", "arbitrary", "arbitrary"),
        name="matmul_residual",
    )(a, w, res)


def _mm_rope_kernel(a_ref, w_ref, cos_ref, sin_ref, o_ref, *, n_rope_tiles):
    j = pl.program_id(1)
    acc = jnp.dot(a_ref[...], w_ref[...], preferred_element_type=F32)

    @pl.when(j < n_rope_tiles)
    def _():
        cos = cos_ref[...]
        sin = sin_ref[...]
        for g in range(acc.shape[1] // HEAD_DIM):
            sl = slice(g * HEAD_DIM, (g + 1) * HEAD_DIM)
            t = acc[:, sl]
            o_ref[:, sl] = (t * cos + pltpu.roll(t, HEAD_DIM // 2, 1) * sin).astype(o_ref.dtype)

    @pl.when(j >= n_rope_tiles)
    def _():
        o_ref[...] = acc.astype(o_ref.dtype)


def matmul_rope(a, w, cos, sin_signed, seq, rope_cols, tm=1024, tn=1024):
    m, k = a.shape
    n = w.shape[1]
    nseq = seq // tm
    return pl.pallas_call(
        functools.partial(_mm_rope_kernel, n_rope_tiles=rope_cols // tn),
        grid=(m // tm, n // tn),
        in_specs=[pl.BlockSpec((tm, k), lambda i, j: (i, 0)),
                  pl.BlockSpec((k, tn), lambda i, j: (0, j)),
                  pl.BlockSpec((tm, HEAD_DIM), lambda i, j: (i % nseq, 0)),
                  pl.BlockSpec((tm, HEAD_DIM), lambda i, j: (i % nseq, 0))],
        out_specs=pl.BlockSpec((tm, tn), lambda i, j: (i, j)),
        out_shape=jax.ShapeDtypeStruct((m, n), BF16),
        compiler_params=_params("parallel", "arbitrary"),
        name="matmul_rope",
    )(a, w, cos, sin_signed)


def _diff_attn_kernel(lq1_ref, lk1_ref, lq2_ref, lk2_ref, sw_ref, q_ref, k_ref, v_ref, o_ref,
                      acc_ref, m_ref, l_ref, *, tq, scale, lambda_init):
    i = pl.program_id(2)
    q = q_ref[...]
    qs = (q[:, :HEAD_DIM], q[:, HEAD_DIM:])
    m_ref[...] = jnp.full(m_ref.shape, -jnp.inf, F32)
    l_ref[...] = jnp.zeros(l_ref.shape, F32)
    acc_ref[...] = jnp.zeros(acc_ref.shape, F32)

    def step(j, mask):
        off = pl.multiple_of(j * tq, tq)
        k = k_ref[pl.ds(off, tq), :]
        v = v_ref[pl.ds(off, tq), :]
        for a in range(2):
            ka = k[:, a * HEAD_DIM:(a + 1) * HEAD_DIM]
            s = lax.dot_general(qs[a], ka, (((1,), (1,)), ((), ())),
                                preferred_element_type=F32) * scale
            if mask is not None:
                s = jnp.where(mask, s, -jnp.inf)
            m_prev = m_ref[a]
            m_new = jnp.maximum(m_prev, jnp.max(s, axis=-1, keepdims=True))
            alpha = jnp.exp(m_prev - m_new)
            p = jnp.exp(s - m_new)
            l_ref[a] = alpha * l_ref[a] + jnp.sum(p, axis=-1, keepdims=True)
            acc_ref[a] = alpha * acc_ref[a] + jnp.dot(p.astype(BF16), v, preferred_element_type=F32)
            m_ref[a] = m_new

    def body(j, c):
        step(j, None)
        return c

    lax.fori_loop(0, i, body, 0)
    row = lax.broadcasted_iota(jnp.int32, (tq, tq), 0) // CHUNK
    col = lax.broadcasted_iota(jnp.int32, (tq, tq), 1) // CHUNK
    step(i, col <= row)

    lam = (jnp.exp(jnp.sum(lq1_ref[...] * lk1_ref[...], axis=-1, keepdims=True))
           - jnp.exp(jnp.sum(lq2_ref[...] * lk2_ref[...], axis=-1, keepdims=True)) + lambda_init)
    o = acc_ref[0] / l_ref[0] - lam * (acc_ref[1] / l_ref[1])
    y = o * lax.rsqrt(jnp.mean(o * o, axis=-1, keepdims=True) + EPS)
    o_ref[...] = ((y * sw_ref[...]) * (1.0 - lambda_init)).astype(o_ref.dtype)


def diff_attention(proj, lq1, lk1, lq2, lk2, subln_w, *, n_heads, q_col0, k_col0, v_col0,
                   lambda_init, tq=256):
    b, s, _ = proj.shape
    w = 2 * HEAD_DIM
    vec = lambda t: t.reshape(1, -1).astype(F32)
    small = lambda n: pl.BlockSpec((1, n), lambda bb, h, i: (0, 0))
    return pl.pallas_call(
        functools.partial(_diff_attn_kernel, tq=tq, scale=HEAD_DIM ** -0.5, lambda_init=lambda_init),
        grid=(b, n_heads, s // tq),
        in_specs=[small(HEAD_DIM), small(HEAD_DIM), small(HEAD_DIM), small(HEAD_DIM), small(w),
                  pl.BlockSpec((None, tq, w), lambda bb, h, i: (bb, i, q_col0 + h)),
                  pl.BlockSpec((None, s, w), lambda bb, h, i: (bb, 0, k_col0 + h)),
                  pl.BlockSpec((None, s, w), lambda bb, h, i: (bb, 0, v_col0 + h))],
        out_specs=pl.BlockSpec((None, tq, w), lambda bb, h, i: (bb, i, h)),
        out_shape=jax.ShapeDtypeStruct((b, s, n_heads * w), BF16),
        scratch_shapes=[pltpu.VMEM((2, tq, w), F32),
                        pltpu.VMEM((2, tq, 1), F32),
                        pltpu.VMEM((2, tq, 1), F32)],
        compiler_params=_params("parallel", "parallel", "arbitrary"),
        name="diff_attention",
    )(vec(lq1), vec(lk1), vec(lq2), vec(lk2), vec(subln_w), proj, proj, proj)


def _band_attn_kernel(bias_ref, q_ref, k_ref, v_ref, o_ref, *, tq, scale):
    i = pl.program_id(2)
    nb = LEFT_CHUNKS * CHUNK // tq
    q = q_ref[...]
    rc = lax.broadcasted_iota(jnp.int32, (tq, tq), 0) // CHUNK
    cc = lax.broadcasted_iota(jnp.int32, (tq, tq), 1) // CHUNK
    ss, vs = [], []
    for j in range(nb + 1):
        kb = i - nb + j
        off = pl.multiple_of(jnp.maximum(kb, 0) * tq, tq)
        k = k_ref[pl.ds(off, tq), :]
        vs.append(v_ref[pl.ds(off, tq), :])
        s = lax.dot_general(q, k, (((1,), (1,)), ((), ())), preferred_element_type=F32) * scale
        s = s + bias_ref[:, j * tq:(j + 1) * tq]
        if j == 0:
            ok = (cc >= rc) & (kb >= 0)
        elif j == nb:
            ok = cc <= rc
        else:
            ok = jnp.broadcast_to(kb >= 0, (tq, tq))
        ss.append(jnp.where(ok, s, -jnp.inf))
    m = ss[0].max(axis=-1, keepdims=True)
    for s in ss[1:]:
        m = jnp.maximum(m, s.max(axis=-1, keepdims=True))
    l = jnp.zeros((tq, 1), F32)
    acc = jnp.zeros((tq, HEAD_DIM), F32)
    for s, v in zip(ss, vs):
        p = jnp.exp(s - m)
        l = l + p.sum(axis=-1, keepdims=True)
        acc = acc + jnp.dot(p.astype(BF16), v, preferred_element_type=F32)
    o_ref[...] = (acc / l).astype(o_ref.dtype)


def band_bias_tiles(rel_bias, tq):
    left = LEFT_CHUNKS * CHUNK
    rel = jnp.arange(tq + left)[None, :] - left - jnp.arange(tq)[:, None]
    idx = jnp.clip(rel, -REL_CLIP, REL_CLIP) + REL_CLIP
    return rel_bias.astype(F32)[:, idx]


def band_attention(proj, bias_tiles, *, n_heads, q_col0, k_col0, v_col0, tq=256):
    b, s, _ = proj.shape
    wk = bias_tiles.shape[-1]
    return pl.pallas_call(
        functools.partial(_band_attn_kernel, tq=tq, scale=HEAD_DIM ** -0.5),
        grid=(b, n_heads, s // tq),
        in_specs=[pl.BlockSpec((None, tq, wk), lambda bb, h, i: (h, 0, 0)),
                  pl.BlockSpec((None, tq, HEAD_DIM), lambda bb, h, i: (bb, i, q_col0 + h)),
                  pl.BlockSpec((None, s, HEAD_DIM), lambda bb, h, i: (bb, 0, k_col0 + h)),
                  pl.BlockSpec((None, s, HEAD_DIM), lambda bb, h, i: (bb, 0, v_col0 + h))],
        out_specs=pl.BlockSpec((None, tq, HEAD_DIM), lambda bb, h, i: (bb, i, h)),
        out_shape=jax.ShapeDtypeStruct((b, s, n_heads * HEAD_DIM), BF16),
        compiler_params=_params("parallel", "parallel", "arbitrary"),
        name="band_attention",
    )(bias_tiles, proj, proj, proj)


def _mem_attn_kernel(q_ref, k_ref, v_ref, o_ref, *, scale):
    s = lax.dot_general(q_ref[...], k_ref[...], (((1,), (1,)), ((), ())),
                        preferred_element_type=F32) * scale
    m = s.max(axis=-1, keepdims=True)
    p = jnp.exp(s - m)
    l = p.sum(axis=-1, keepdims=True)
    o = jnp.dot(p.astype(BF16), v_ref[...], preferred_element_type=F32)
    o_ref[...] = (o / l).astype(o_ref.dtype)


def mem_attention(q, kv, *, n_heads, tq=512):
    b, s, d = q.shape
    mlen = kv.shape[1]
    dh = d // n_heads
    return pl.pallas_call(
        functools.partial(_mem_attn_kernel, scale=dh ** -0.5),
        grid=(b, n_heads, s // tq),
        in_specs=[pl.BlockSpec((None, tq, dh), lambda bb, h, i: (bb, i, h)),
                  pl.BlockSpec((None, mlen, dh), lambda bb, h, i: (bb, 0, h)),
                  pl.BlockSpec((None, mlen, dh), lambda bb, h, i: (bb, 0, n_heads + h))],
        out_specs=pl.BlockSpec((None, tq, dh), lambda bb, h, i: (bb, i, h)),
        out_shape=jax.ShapeDtypeStruct((b, s, d), BF16),
        compiler_params=_params("parallel", "parallel", "arbitrary"),
        name="mem_attention",
    )(q, kv, kv)


def _ffn_up_kernel(a_ref, wg_ref, wu_ref, cw_ref, cb_ref, o_ref, carry_ref, *, tiles_per_seq):
    i = pl.program_id(0)
    j = pl.program_id(1)
    a = a_ref[...]
    g = jnp.dot(a, wg_ref[...], preferred_element_type=F32)
    u = jnp.dot(a, wu_ref[...], preferred_element_type=F32)
    tm = g.shape[0]
    @pl.when(i % tiles_per_seq == 0)
    def _():
        carry_ref[j] = jnp.zeros(carry_ref.shape[1:], F32)

    prev = carry_ref[j]
    carry_ref[j] = g[tm - 8:, :]
    row = lax.broadcasted_iota(jnp.int32, g.shape, 0)
    g1 = jnp.where(row == 0, prev[7:8, :], pltpu.roll(g, 1, 0))
    g2 = jnp.where(row == 0, prev[6:7, :],
                   jnp.where(row == 1, prev[7:8, :], pltpu.roll(g, 2, 0)))
    cw = cw_ref[...]
    gc = cw[0:1, :] * g2 + cw[1:2, :] * g1 + cw[2:3, :] * g + cb_ref[...]
    o_ref[...] = (gc * jax.nn.sigmoid(gc) * u).astype(o_ref.dtype)


def ffn_up(a, w_up, conv_w, conv_b, seq, tm=1024, tn=256):
    m, d = a.shape
    f = w_up.shape[1] // 2
    nj = f // tn
    return pl.pallas_call(
        functools.partial(_ffn_up_kernel, tiles_per_seq=seq // tm),
        grid=(m // tm, nj),
        in_specs=[pl.BlockSpec((tm, d), lambda i, j: (i, 0)),
                  pl.BlockSpec((d, tn), lambda i, j: (0, j)),
                  pl.BlockSpec((d, tn), lambda i, j: (0, nj + j)),
                  pl.BlockSpec((CONV_W, tn), lambda i, j: (0, j)),
                  pl.BlockSpec((1, tn), lambda i, j: (0, j))],
        out_specs=pl.BlockSpec((tm, tn), lambda i, j: (i, j)),
        out_shape=jax.ShapeDtypeStruct((m, f), BF16),
        scratch_shapes=[pltpu.VMEM((nj, 8, tn), F32)],
        compiler_params=_params("arbitrary", "arbitrary"),
        name="ffn_up",
    )(a, w_up, w_up, conv_w.astype(F32), conv_b.reshape(1, f).astype(F32))


def _rope_tables(seq, dim):
    inv = 1.0 / (ROPE_THETA ** (jnp.arange(0, dim, 2, dtype=F32) / dim))
    ang = jnp.arange(seq, dtype=F32)[:, None] * inv[None, :]
    ang = jnp.concatenate([ang, ang], axis=-1)
    sign = jnp.where(jnp.arange(dim) < dim // 2, -1.0, 1.0).astype(F32)
    return jnp.cos(ang), jnp.sin(ang) * sign[None, :]


def kernel(x, mem, norm_mix_w, w_in, lambda_q1, lambda_k1, lambda_q2, lambda_k2, subln_w, rel_bias, w_out,
           norm_xq_w, norm_mem_w, w_mq, w_mkv, w_mo, norm_ffn_w, w_up, conv_w, conv_b, w_down, final_norm_w):
    b, s, d = x.shape
    depth = w_in.shape[0]
    width_a = d // 2
    n_diff = width_a // (2 * HEAD_DIM)
    n_band = (d - width_a) // HEAD_DIM
    mlen = mem.shape[1]
    cos, sin_signed = _rope_tables(s, HEAD_DIM)
    xf = x.reshape(b * s, d)
    memf = mem.reshape(b * mlen, d)
    for l in range(depth):
        lambda_init = 0.8 - 0.6 * math.exp(-0.3 * l)
        h = rmsnorm(xf, norm_mix_w[l], BF16)
        proj = matmul_rope(h, w_in[l].astype(BF16), cos, sin_signed, s, 2 * width_a).reshape(b, s, -1)
        oa = diff_attention(proj, lambda_q1[l], lambda_k1[l], lambda_q2[l], lambda_k2[l], subln_w[l],
                            n_heads=n_diff, q_col0=0, k_col0=n_diff, v_col0=2 * n_diff,
                            lambda_init=lambda_init)
        band0 = 3 * width_a // HEAD_DIM
        ob = band_attention(proj, band_bias_tiles(rel_bias[l], 256), n_heads=n_band,
                            q_col0=band0, k_col0=band0 + n_band, v_col0=band0 + 2 * n_band)
        mix = jnp.concatenate([oa, ob], axis=-1).reshape(b * s, d)
        xf = matmul_residual(mix, w_out[l].astype(BF16), xf)
        hq = rmsnorm(xf, norm_xq_w[l], BF16)
        mem_n = rmsnorm(memf, norm_mem_w[l], BF16)
        q = matmul(hq, w_mq[l].astype(BF16), BF16).reshape(b, s, d)
        kv = matmul(mem_n, w_mkv[l].astype(BF16), BF16).reshape(b, mlen, 2 * d)
        om = mem_attention(q, kv, n_heads=N_MEM_HEADS).reshape(b * s, d)
        xf = matmul_residual(om, w_mo[l].astype(BF16), xf)
        hf = rmsnorm(xf, norm_ffn_w[l], BF16)
        hid = ffn_up(hf, w_up[l].astype(BF16), conv_w[l], conv_b[l], s)
        xf = matmul_residual(hid, w_down[l].astype(BF16), xf, tk=hid.shape[1] // 2)
    return rmsnorm(xf, final_norm_w, x.dtype).reshape(b, s, d)
```
